```python
import jax, jax.numpy as jnp
from jax import lax
import numpy as np

D_MODEL = 1024
BATCH = 16
SEQ = 4096
DEPTH = 1
DEC_BATCH = 32
DEC_SEQ = 64
PAST_LEN = 4096

CHUNK = 64
D_MIX = D_MODEL
D_A = D_MIX // 2
D_B = D_MIX - D_A
D_IN = 2 * D_A + 3 * D_B
CONV_A_WIDTH = 31
CONV_B_WIDTH = 3
D_FF = -(-8 * D_MODEL // (3 * 256)) * 256
EPS = 1e-6

kernel_name = "hybrid_conformer_shortconv_stream_step"


def rms_norm(x, g):
    x32 = x.astype(jnp.float32)
    y = x32 * lax.rsqrt(jnp.mean(x32 * x32, axis=-1, keepdims=True) + EPS)
    return (y * g.astype(jnp.float32)).astype(x.dtype)


def layer_norm(x, g, b):
    x32 = x.astype(jnp.float32)
    mu = jnp.mean(x32, axis=-1, keepdims=True)
    xc = x32 - mu
    y = xc * lax.rsqrt(jnp.mean(xc * xc, axis=-1, keepdims=True) + EPS)
    return (y * g.astype(jnp.float32) + b.astype(jnp.float32)).astype(x.dtype)


def causal_dwconv(u, hist, w):
    k = w.shape[0]
    up = jnp.concatenate([hist.astype(u.dtype), u], axis=1)
    y = lax.conv_general_dilated(
        up, w[:, None, :].astype(u.dtype), window_strides=(1,), padding="VALID",
        dimension_numbers=("NWC", "WIO", "NWC"), feature_group_count=u.shape[-1])
    return y, up[:, up.shape[1] - (k - 1):, :]


def mixer(h, hist_a, hist_b, w_in, conv_a_w, conv_a_b, ln_g, ln_b, conv_b_w, w_out):
    p = jnp.einsum("btd,de->bte", h, w_in)
    a_val, a_gate, g_b, g_c, v = jnp.split(
        p, [D_A, 2 * D_A, 2 * D_A + D_B, 2 * D_A + 2 * D_B], axis=-1)
    a = a_val * jax.nn.sigmoid(a_gate)
    a, new_a = causal_dwconv(a, hist_a, conv_a_w)
    a = jax.nn.silu(layer_norm(a + conv_a_b, ln_g, ln_b))
    u, new_b = causal_dwconv(g_c * v, hist_b, conv_b_w)
    bo = g_b * u
    out = jnp.einsum("bte,ed->btd", jnp.concatenate([a, bo], axis=-1), w_out)
    return out, new_a, new_b


def swiglu(h, w_gate_up, w_down):
    gu = jnp.einsum("btd,df->btf", h, w_gate_up)
    g, u = jnp.split(gu, 2, axis=-1)
    return jnp.einsum("btf,fd->btd", jax.nn.silu(g) * u, w_down)


def setup_inputs(seed: int = 0) -> dict:
    key = jax.random.key(seed)
    ks = jax.random.split(key, 20)
    n = jax.random.normal
    f32 = jnp.float32
    return {
        "x_prompt": n(ks[0], (BATCH, SEQ, D_MODEL), f32),
        "x_sample": n(ks[1], (DEC_BATCH, DEC_SEQ, D_MODEL), f32),
        "cache_conv_a": 0.5 * n(ks[2], (DEPTH, DEC_BATCH, CONV_A_WIDTH - 1, D_A), f32),
        "cache_conv_b": 0.5 * n(ks[3], (DEPTH, DEC_BATCH, CONV_B_WIDTH - 1, D_B), f32),
        "norm_mix_pre": 1.0 + 0.01 * n(ks[4], (DEPTH, D_MODEL), f32),
        "norm_mix_post": 1.0 + 0.01 * n(ks[5], (DEPTH, D_MODEL), f32),
        "w_in": n(ks[6], (DEPTH, D_MODEL, D_IN), f32) * D_MODEL ** -0.5,
        "conv_a_w": n(ks[7], (DEPTH, CONV_A_WIDTH, D_A), f32) * CONV_A_WIDTH ** -0.5,
        "conv_a_b": 0.01 * n(ks[8], (DEPTH, D_A), f32),
        "conv_a_ln_g": 1.0 + 0.01 * n(ks[9], (DEPTH, D_A), f32),
        "conv_a_ln_b": 0.01 * n(ks[10], (DEPTH, D_A), f32),
        "conv_b_w": n(ks[11], (DEPTH, CONV_B_WIDTH, D_B), f32) * CONV_B_WIDTH ** -0.5,
        "w_out": n(ks[12], (DEPTH, D_MIX, D_MODEL), f32) * D_MIX ** -0.5,
        "norm_ffn_pre": 1.0 + 0.01 * n(ks[13], (DEPTH, D_MODEL), f32),
        "norm_ffn_post": 1.0 + 0.01 * n(ks[14], (DEPTH, D_MODEL), f32),
        "w_gate_up": n(ks[15], (DEPTH, D_MODEL, 2 * D_FF), f32) * D_MODEL ** -0.5,
        "w_down": n(ks[16], (DEPTH, D_FF, D_MODEL), f32) * D_FF ** -0.5,
    }


def reference(x_prompt, x_sample, cache_conv_a, cache_conv_b, norm_mix_pre, norm_mix_post,
              w_in, conv_a_w, conv_a_b, conv_a_ln_g, conv_a_ln_b, conv_b_w, w_out,
              norm_ffn_pre, norm_ffn_post, w_gate_up, w_down):
    def run(x, hists_a, hists_b):
        new_as, new_bs = [], []
        for l in range(DEPTH):
            m, na, nb = mixer(rms_norm(x, norm_mix_pre[l]), hists_a[l], hists_b[l],
                              w_in[l], conv_a_w[l], conv_a_b[l], conv_a_ln_g[l],
                              conv_a_ln_b[l], conv_b_w[l], w_out[l])
            x = x + rms_norm(m, norm_mix_post[l])
            f = swiglu(rms_norm(x, norm_ffn_pre[l]), w_gate_up[l], w_down[l])
            x = x + rms_norm(f, norm_ffn_post[l])
            new_as.append(na)
            new_bs.append(nb)
        return x, jnp.stack(new_as), jnp.stack(new_bs)

    nb_p = x_prompt.shape[0]
    zeros_a = jnp.zeros((DEPTH, nb_p, CONV_A_WIDTH - 1, D_A), x_prompt.dtype)
    zeros_b = jnp.zeros((DEPTH, nb_p, CONV_B_WIDTH - 1, D_B), x_prompt.dtype)
    y_prompt, conv_a_prompt, conv_b_prompt = run(x_prompt, zeros_a, zeros_b)
    y_sample, conv_a_sample, conv_b_sample = run(x_sample, cache_conv_a, cache_conv_b)
    return (y_prompt, y_sample, conv_a_prompt, conv_b_prompt, conv_a_sample, conv_b_sample)
```

```python
import functools

import jax
import jax.numpy as jnp
from jax import lax
from jax.experimental import pallas as pl
from jax.experimental.pallas import tpu as pltpu

EPS = 1e-6
CONV_A_WIDTH = 31
CONV_B_WIDTH = 3
HIST_A = CONV_A_WIDTH - 1
HIST_B = CONV_B_WIDTH - 1
SUBLANES = 8
LANES = 128
HEAD_A = 32
HEAD_B = 8
CONV_ROWS = 32
MXU_COLS = 256
VMEM_LIMIT_BYTES = 60 * 1024 * 1024


def _rms(x, g):
    ms = jnp.mean(x * x, axis=-1, keepdims=True)
    return x * lax.rsqrt(ms + EPS) * g


def _shifted_taps(buf, w8_ref, base, rows, lanes, head, width):
    first = head - (width - 1)
    acc = None
    for r in range(SUBLANES):
        span = rows + (SUBLANES if r else 0)
        part = None
        for q in range((head // SUBLANES) + 1):
            k = SUBLANES * q + r - first
            if k < 0 or k >= width:
                continue
            slab = buf[pl.ds(base + SUBLANES * q, span), lanes]
            term = slab.reshape(span // SUBLANES, SUBLANES, LANES) * w8_ref[k][:, lanes][None]
            part = term if part is None else part + term
        if part is None:
            continue
        sh = part.reshape(span, LANES)[r:r + rows]
        acc = sh if acc is None else acc + sh
    return acc


def _layer_kernel(x_ref, ha_ref, hb_ref, g1_ref, g2_ref, win_ref, caw_ref, cab_ref, lng_ref,
                  lnb_ref, cbw_ref, wout_ref, g3_ref, g4_ref, wgu_ref, wd_ref,
                  y_ref, na_ref, nb_ref,
                  pbuf, abuf, cbuf, hbuf, mixbuf, actbuf, *, ff_chunks):
    j = pl.program_id(1)
    nb, T, D = x_ref.shape
    R = nb * T
    d_a = abuf.shape[1]
    d_b = cbuf.shape[1]
    d_ff = wd_ref.shape[0]
    sa = HEAD_A + T
    sb = HEAD_B + T
    f32 = jnp.float32
    bf16 = jnp.bfloat16

    x = x_ref[...].reshape(R, D)
    hbuf[...] = _rms(x, g1_ref[...]).astype(bf16)
    for n0 in range(0, win_ref.shape[1], 2 * MXU_COLS):
        pbuf[:, n0:n0 + 2 * MXU_COLS] = jnp.dot(
            hbuf[...], win_ref[:, n0:n0 + 2 * MXU_COLS], preferred_element_type=f32)

    @pl.when(j == 0)
    def _():
        for s in range(nb):
            abuf[s * sa:s * sa + HEAD_A, :] = ha_ref[s]
            cbuf[s * sb:s * sb + HEAD_B, :] = hb_ref[s]

    @pl.when(j > 0)
    def _():
        for s in range(nb):
            abuf[s * sa:s * sa + HEAD_A, :] = abuf[s * sa + T:s * sa + T + HEAD_A, :]
            cbuf[s * sb:s * sb + HEAD_B, :] = cbuf[s * sb + T:s * sb + T + HEAD_B, :]

    for s in range(nb):
        rows = slice(s * T, (s + 1) * T)
        abuf[s * sa + HEAD_A:(s + 1) * sa, :] = (
            pbuf[rows, 0:d_a] * jax.nn.sigmoid(pbuf[rows, d_a:2 * d_a]))
        cbuf[s * sb + HEAD_B:(s + 1) * sb, :] = (
            pbuf[rows, 2 * d_a + d_b:2 * d_a + 2 * d_b] * pbuf[rows, 2 * d_a + 2 * d_b:2 * d_a + 3 * d_b])
        na_ref[s] = abuf[(s + 1) * sa - HIST_A:(s + 1) * sa, :]
        nb_ref[s] = cbuf[(s + 1) * sb - HIST_B:(s + 1) * sb, :]

    cps = T // CONV_ROWS

    def conv_chunk(idx, carry):
        s = idx // cps
        ch = idx % cps
        base_a = pl.multiple_of(s * sa + ch * CONV_ROWS, SUBLANES)
        base_b = pl.multiple_of(s * sb + ch * CONV_ROWS, SUBLANES)
        row0 = pl.multiple_of(s * T + ch * CONV_ROWS, SUBLANES)
        ya = []
        for c in range(d_a // LANES):
            lanes = slice(c * LANES, (c + 1) * LANES)
            ya.append(_shifted_taps(abuf, caw_ref, base_a, CONV_ROWS, lanes, HEAD_A, CONV_A_WIDTH))
        ya = jnp.concatenate(ya, axis=1) + cab_ref[...]
        mu = jnp.mean(ya, axis=-1, keepdims=True)
        yc = ya - mu
        ln = yc * lax.rsqrt(jnp.mean(yc * yc, axis=-1, keepdims=True) + EPS)
        ln = ln * lng_ref[...] + lnb_ref[...]
        mixbuf[pl.ds(row0, CONV_ROWS), 0:d_a] = (ln * jax.nn.sigmoid(ln)).astype(bf16)
        yb = []
        for c in range(d_b // LANES):
            lanes = slice(c * LANES, (c + 1) * LANES)
            yb.append(_shifted_taps(cbuf, cbw_ref, base_b, CONV_ROWS, lanes, HEAD_B, CONV_B_WIDTH))
        yb = jnp.concatenate(yb, axis=1)
        gate_b = pbuf[pl.ds(row0, CONV_ROWS), 2 * d_a:2 * d_a + d_b]
        mixbuf[pl.ds(row0, CONV_ROWS), d_a:d_a + d_b] = (gate_b * yb).astype(bf16)
        return carry

    lax.fori_loop(0, nb * cps, conv_chunk, 0)

    m = jnp.dot(mixbuf[...], wout_ref[...], preferred_element_type=f32)
    y_ref[...] = (x_ref[...].reshape(R, D) + _rms(m, g2_ref[...])).reshape(nb, T, D)

    hbuf[...] = _rms(y_ref[...].reshape(R, D), g3_ref[...]).astype(bf16)
    c0 = 0
    for fc in ff_chunks:
        g = jnp.dot(hbuf[...], wgu_ref[:, c0:c0 + fc], preferred_element_type=f32)
        u = jnp.dot(hbuf[...], wgu_ref[:, d_ff + c0:d_ff + c0 + fc], preferred_element_type=f32)
        actbuf[:, c0:c0 + fc] = (g * jax.nn.sigmoid(g) * u).astype(bf16)
        c0 += fc
    f = jnp.dot(actbuf[...], wd_ref[...], preferred_element_type=f32)
    y_ref[...] = y_ref[...] + _rms(f, g4_ref[...]).reshape(nb, T, D)


def _ff_chunks(d_ff):
    chunks = []
    left = d_ff
    while left > 0:
        fc = min(2 * MXU_COLS, left)
        chunks.append(fc)
        left -= fc
    return tuple(chunks)


def _resident(shape):
    return pl.BlockSpec(shape, lambda i, j: (0,) * len(shape), pipeline_mode=pl.Buffered(1))


def _layer(x, hist_a, hist_b, p, *, seqs_per_block, rows_per_seq):
    B, S, D = x.shape
    nb, T = seqs_per_block, rows_per_seq
    assert B % nb == 0 and S % T == 0 and T % CONV_ROWS == 0 and T >= HEAD_A
    d_a = hist_a.shape[-1]
    d_b = hist_b.shape[-1]
    d_in = p["w_in"].shape[1]
    d_ff = p["w_down"].shape[0]
    assert d_in % (2 * MXU_COLS) == 0 and d_a % LANES == 0 and d_b % LANES == 0
    R = nb * T
    f32 = jnp.float32
    bf16 = jnp.bfloat16

    ha = jnp.pad(hist_a, ((0, 0), (HEAD_A - HIST_A, 0), (0, 0)))
    hb = jnp.pad(hist_b, ((0, 0), (HEAD_B - HIST_B, 0), (0, 0)))
    row = lambda v: v.reshape(1, -1)
    bcast8 = lambda w: jnp.broadcast_to(w[:, None, :], (w.shape[0], SUBLANES, w.shape[1]))

    args = (
        x, ha, hb, row(p["norm_mix_pre"]), row(p["norm_mix_post"]), p["w_in"].astype(bf16),
        bcast8(p["conv_a_w"]), row(p["conv_a_b"]), row(p["conv_a_ln_g"]), row(p["conv_a_ln_b"]),
        bcast8(p["conv_b_w"]), p["w_out"].astype(bf16), row(p["norm_ffn_pre"]),
        row(p["norm_ffn_post"]), p["w_gate_up"].astype(bf16), p["w_down"].astype(bf16))

    seq_blk = lambda shape: pl.BlockSpec(shape, lambda i, j: (i, 0, 0))
    in_specs = [
        pl.BlockSpec((nb, T, D), lambda i, j: (i, j, 0)),
        seq_blk((nb, HEAD_A, d_a)), seq_blk((nb, HEAD_B, d_b)),
    ] + [_resident(a.shape) for a in args[3:]]
    out_specs = [
        pl.BlockSpec((nb, T, D), lambda i, j: (i, j, 0)),
        seq_blk((nb, HIST_A, d_a)), seq_blk((nb, HIST_B, d_b)),
    ]
    out_shape = [
        jax.ShapeDtypeStruct((B, S, D), x.dtype),
        jax.ShapeDtypeStruct((B, HIST_A, d_a), x.dtype),
        jax.ShapeDtypeStruct((B, HIST_B, d_b), x.dtype),
    ]
    scratch = [
        pltpu.VMEM((R, d_in), f32),
        pltpu.VMEM((nb * (HEAD_A + T), d_a), f32),
        pltpu.VMEM((nb * (HEAD_B + T), d_b), f32),
        pltpu.VMEM((R, D), bf16),
        pltpu.VMEM((R, d_a + d_b), bf16),
        pltpu.VMEM((R, d_ff), bf16),
    ]
    return pl.pallas_call(
        functools.partial(_layer_kernel, ff_chunks=_ff_chunks(d_ff)),
        grid=(B // nb, S // T),
        in_specs=in_specs, out_specs=out_specs, out_shape=out_shape,
        scratch_shapes=scratch,
        compiler_params=pltpu.CompilerParams(
            dimension_semantics=("arbitrary", "arbitrary"),
            vmem_limit_bytes=VMEM_LIMIT_BYTES),
        name="encoder_layer",
    )(*args)


def _run(x, hists_a, hists_b, params, *, seqs_per_block, rows_per_seq):
    depth = hists_a.shape[0]
    new_as, new_bs = [], []
    for l in range(depth):
        p = {k: v[l] for k, v in params.items()}
        x, na, nb = _layer(x, hists_a[l], hists_b[l], p,
                           seqs_per_block=seqs_per_block, rows_per_seq=rows_per_seq)
        new_as.append(na)
        new_bs.append(nb)
    return x, jnp.stack(new_as), jnp.stack(new_bs)


def kernel(x_prompt, x_sample, cache_conv_a, cache_conv_b, norm_mix_pre, norm_mix_post, w_in, conv_a_w, conv_a_b, conv_a_ln_g, conv_a_ln_b, conv_b_w, w_out, norm_ffn_pre, norm_ffn_post, w_gate_up, w_down):
    params = dict(
        norm_mix_pre=norm_mix_pre, norm_mix_post=norm_mix_post, w_in=w_in, conv_a_w=conv_a_w,
        conv_a_b=conv_a_b, conv_a_ln_g=conv_a_ln_g, conv_a_ln_b=conv_a_ln_b, conv_b_w=conv_b_w,
        w_out=w_out, norm_ffn_pre=norm_ffn_pre, norm_ffn_post=norm_ffn_post,
        w_gate_up=w_gate_up, w_down=w_down)
    depth = cache_conv_a.shape[0]
    bp = x_prompt.shape[0]
    zeros_a = jnp.zeros((depth, bp) + cache_conv_a.shape[2:], x_prompt.dtype)
    zeros_b = jnp.zeros((depth, bp) + cache_conv_b.shape[2:], x_prompt.dtype)
    y_p, a_p, b_p = _run(x_prompt, zeros_a, zeros_b, params, seqs_per_block=1, rows_per_seq=512)
    y_s, a_s, b_s = _run(x_sample, cache_conv_a, cache_conv_b, params,
                         seqs_per_block=8, rows_per_seq=x_sample.shape[1])
    return (y_p, y_s, a_p, b_p, a_s, b_s)
```

```python
import functools

import jax
import jax.numpy as jnp
from jax import lax
from jax.experimental import pallas as pl
from jax.experimental.pallas import tpu as pltpu

EPS = 1e-6
CONV_A_WIDTH = 31
CONV_B_WIDTH = 3
HIST_A = CONV_A_WIDTH - 1
HIST_B = CONV_B_WIDTH - 1
SUBLANES = 8
LANES = 128
HEAD_A = 32
HEAD_B = 8
CONV_ROWS = 32
MXU_COLS = 256
VMEM_LIMIT_BYTES = 60 * 1024 * 1024


def _rms(x, g):
    ms = jnp.mean(x * x, axis=-1, keepdims=True)
    return x * lax.rsqrt(ms + EPS) * g


def _taps(buf, blk, w8_ref, base, rows, lanes, width):
    acc = None
    for k in range(width):
        slab = buf[blk, pl.ds(base + k, rows), :].reshape(rows // SUBLANES, SUBLANES, LANES)
        term = slab * w8_ref[k][:, lanes][None]
        acc = term if acc is None else acc + term
    return acc.reshape(rows, LANES)


def _layer_kernel(xc_ref, xp_ref, ha_ref, hb_ref, g1_ref, g2_ref, win_ref, caw_ref, cab_ref,
                  lng_ref, lnb_ref, cbw_ref, wout_ref, g3_ref, g4_ref, wgu_ref, wd_ref,
                  y_ref, na_ref, nb_ref,
                  pbuf, abuf, cbuf, hbuf, mixbuf, actbuf, *, ff_chunks, n_tiles, tiles_per_seq):
    n = pl.program_id(0)
    slot = n % 2
    nb, T, D = xc_ref.shape
    R = nb * T
    d_a = ha_ref.shape[2]
    d_b = hb_ref.shape[2]
    la = d_a // LANES
    lb = d_b // LANES
    d_ff = wd_ref.shape[0]
    f32 = jnp.float32
    bf16 = jnp.bfloat16

    def project_in():
        hbuf[...] = _rms(xc_ref[...].reshape(R, D), g1_ref[...]).astype(bf16)
        for n0 in range(0, win_ref.shape[1], 2 * MXU_COLS):
            pbuf[:, n0:n0 + 2 * MXU_COLS] = jnp.dot(
                hbuf[...], win_ref[:, n0:n0 + 2 * MXU_COLS], preferred_element_type=f32)

    cps = T // CONV_ROWS
    n_chunks = nb * cps

    def headers(seq_start):
        for s in range(nb):
            for c in range(la):
                head = ha_ref[s][:, c * LANES:(c + 1) * LANES]
                if seq_start is not None:
                    head = jnp.where(seq_start, head, abuf[s * la + c, T:T + HEAD_A, :])
                abuf[s * la + c, 0:HEAD_A, :] = head
            for c in range(lb):
                head = hb_ref[s][:, c * LANES:(c + 1) * LANES]
                if seq_start is not None:
                    head = jnp.where(seq_start, head, cbuf[s * lb + c, T:T + HEAD_B, :])
                cbuf[s * lb + c, 0:HEAD_B, :] = head

    def tails():
        for s in range(nb):
            for c in range(la):
                na_ref[s, :, c * LANES:(c + 1) * LANES] = abuf[s * la + c, HEAD_A + T - HIST_A:HEAD_A + T, :]
            for c in range(lb):
                nb_ref[s, :, c * LANES:(c + 1) * LANES] = cbuf[s * lb + c, HEAD_B + T - HIST_B:HEAD_B + T, :]

    def front_chunk(j):
        s, ch = divmod(j, cps)
        t0 = ch * CONV_ROWS
        rows = slice(s * T + t0, s * T + t0 + CONV_ROWS)
        for c in range(la):
            lanes = slice(c * LANES, (c + 1) * LANES)
            abuf[s * la + c, HEAD_A + t0:HEAD_A + t0 + CONV_ROWS, :] = (
                pbuf[rows, lanes] * jax.nn.sigmoid(pbuf[rows, d_a + c * LANES:d_a + (c + 1) * LANES]))
        for c in range(lb):
            c0 = 2 * d_a + d_b + c * LANES
            cbuf[s * lb + c, HEAD_B + t0:HEAD_B + t0 + CONV_ROWS, :] = (
                pbuf[rows, c0:c0 + LANES] * pbuf[rows, c0 + d_b:c0 + d_b + LANES])

        ya = [_taps(abuf, s * la + c, caw_ref, t0 + HEAD_A - HIST_A, CONV_ROWS,
                    slice(c * LANES, (c + 1) * LANES), CONV_A_WIDTH) for c in range(la)]
        ya = jnp.concatenate(ya, axis=1) + cab_ref[...]
        mu = jnp.mean(ya, axis=-1, keepdims=True)
        yc = ya - mu
        ln = yc * lax.rsqrt(jnp.mean(yc * yc, axis=-1, keepdims=True) + EPS)
        ln = ln * lng_ref[...] + lnb_ref[...]
        mixbuf[slot, rows, 0:d_a] = (ln * jax.nn.sigmoid(ln)).astype(bf16)
        yb = [_taps(cbuf, s * lb + c, cbw_ref, t0 + HEAD_B - HIST_B, CONV_ROWS,
                    slice(c * LANES, (c + 1) * LANES), CONV_B_WIDTH) for c in range(lb)]
        yb = jnp.concatenate(yb, axis=1)
        mixbuf[slot, rows, d_a:d_a + d_b] = (pbuf[rows, 2 * d_a:2 * d_a + d_b] * yb).astype(bf16)

    def front_half(seq_start):
        headers(seq_start)
        for j in range(n_chunks):
            front_chunk(j)
        tails()

    def project_out():
        m = jnp.dot(mixbuf[1 - slot], wout_ref[...], preferred_element_type=f32)
        y_ref[...] = (xp_ref[...].reshape(R, D) + _rms(m, g2_ref[...])).reshape(nb, T, D)
        hbuf[...] = _rms(y_ref[...].reshape(R, D), g3_ref[...]).astype(bf16)

    def ffn():
        c0 = 0
        for fc in ff_chunks:
            g = jnp.dot(hbuf[...], wgu_ref[:, c0:c0 + fc], preferred_element_type=f32)
            u = jnp.dot(hbuf[...], wgu_ref[:, d_ff + c0:d_ff + c0 + fc], preferred_element_type=f32)
            actbuf[:, c0:c0 + fc] = (g * jax.nn.sigmoid(g) * u).astype(bf16)
            c0 += fc
        for h in range(2):
            f = jnp.dot(actbuf[h * (R // 2):(h + 1) * (R // 2), :], wd_ref[...], preferred_element_type=f32)
            out = _rms(f, g4_ref[...])
            if nb == 1:
                half = (0, slice(h * (T // 2), (h + 1) * (T // 2)))
                y_ref[half] = y_ref[half] + out
            else:
                half = slice(h * (nb // 2), (h + 1) * (nb // 2))
                y_ref[half] = y_ref[half] + out.reshape(nb // 2, T, D)

    @pl.when(n == 0)
    def _():
        project_in()
        front_half(None)

    @pl.when(jnp.logical_and(n > 0, n < n_tiles))
    def _():
        project_in()
        project_out()
        ffn()
        front_half(None if tiles_per_seq == 1 else n % tiles_per_seq == 0)

    @pl.when(n == n_tiles)
    def _():
        project_out()
        ffn()


def _ff_chunks(d_ff):
    chunks = []
    left = d_ff
    while left > 0:
        fc = min(2 * MXU_COLS, left)
        chunks.append(fc)
        left -= fc
    return tuple(chunks)


def _resident(shape):
    return pl.BlockSpec(shape, lambda n: (0,) * len(shape), pipeline_mode=pl.Buffered(1))


def _layer(x, hist_a, hist_b, p, *, seqs_per_block, rows_per_seq):
    B, S, D = x.shape
    nb, T = seqs_per_block, rows_per_seq
    assert B % nb == 0 and S % T == 0 and T % CONV_ROWS == 0 and T >= HEAD_A
    d_a = hist_a.shape[-1]
    d_b = hist_b.shape[-1]
    d_in = p["w_in"].shape[1]
    d_ff = p["w_down"].shape[0]
    assert d_in % (2 * MXU_COLS) == 0 and d_a % LANES == 0 and d_b % LANES == 0
    R = nb * T
    tps = S // T
    n_tiles = (B // nb) * tps
    f32 = jnp.float32
    bf16 = jnp.bfloat16

    ha = jnp.pad(hist_a, ((0, 0), (HEAD_A - HIST_A, 0), (0, 0)))
    hb = jnp.pad(hist_b, ((0, 0), (HEAD_B - HIST_B, 0), (0, 0)))
    row = lambda v: v.reshape(1, -1)
    bcast8 = lambda w: jnp.broadcast_to(w[:, None, :], (w.shape[0], SUBLANES, w.shape[1]))

    args = (
        x, x, ha, hb, row(p["norm_mix_pre"]), row(p["norm_mix_post"]), p["w_in"].astype(bf16),
        bcast8(p["conv_a_w"]), row(p["conv_a_b"]), row(p["conv_a_ln_g"]), row(p["conv_a_ln_b"]),
        bcast8(p["conv_b_w"]), p["w_out"].astype(bf16), row(p["norm_ffn_pre"]),
        row(p["norm_ffn_post"]), p["w_gate_up"].astype(bf16), p["w_down"].astype(bf16))

    cur = lambda n: jnp.minimum(n, n_tiles - 1)
    prev = lambda n: jnp.maximum(n - 1, 0)
    cur_tile = lambda n: (cur(n) // tps, cur(n) % tps, 0)
    prev_tile = lambda n: (prev(n) // tps, prev(n) % tps, 0)
    cur_seq = lambda n: (cur(n) // tps, 0, 0)
    in_specs = [
        pl.BlockSpec((nb, T, D), cur_tile),
        pl.BlockSpec((nb, T, D), prev_tile),
        pl.BlockSpec((nb, HEAD_A, d_a), cur_seq),
        pl.BlockSpec((nb, HEAD_B, d_b), cur_seq),
    ] + [_resident(a.shape) for a in args[4:]]
    out_specs = [
        pl.BlockSpec((nb, T, D), prev_tile),
        pl.BlockSpec((nb, HIST_A, d_a), cur_seq),
        pl.BlockSpec((nb, HIST_B, d_b), cur_seq),
    ]
    out_shape = [
        jax.ShapeDtypeStruct((B, S, D), x.dtype),
        jax.ShapeDtypeStruct((B, HIST_A, d_a), x.dtype),
        jax.ShapeDtypeStruct((B, HIST_B, d_b), x.dtype),
    ]
    scratch = [
        pltpu.VMEM((R, d_in), f32),
        pltpu.VMEM((nb * (d_a // LANES), HEAD_A + T, LANES), f32),
        pltpu.VMEM((nb * (d_b // LANES), HEAD_B + T, LANES), f32),
        pltpu.VMEM((R, D), bf16),
        pltpu.VMEM((2, R, d_a + d_b), bf16),
        pltpu.VMEM((R, d_ff), bf16),
    ]
    return pl.pallas_call(
        functools.partial(_layer_kernel, ff_chunks=_ff_chunks(d_ff), n_tiles=n_tiles,
                          tiles_per_seq=tps),
        grid=(n_tiles + 1,),
        in_specs=in_specs, out_specs=out_specs, out_shape=out_shape,
        scratch_shapes=scratch,
        compiler_params=pltpu.CompilerParams(
            dimension_semantics=("arbitrary",),
            vmem_limit_bytes=VMEM_LIMIT_BYTES),
        name="encoder_layer",
    )(*args)


def _run(x, hists_a, hists_b, params, *, seqs_per_block, rows_per_seq):
    depth = hists_a.shape[0]
    new_as, new_bs = [], []
    for l in range(depth):
        p = {k: v[l] for k, v in params.items()}
        x, na, nb = _layer(x, hists_a[l], hists_b[l], p,
                           seqs_per_block=seqs_per_block, rows_per_seq=rows_per_seq)
        new_as.append(na)
        new_bs.append(nb)
    return x, jnp.stack(new_as), jnp.stack(new_bs)


def kernel(x_prompt, x_sample, cache_conv_a, cache_conv_b, norm_mix_pre, norm_mix_post, w_in, conv_a_w, conv_a_b, conv_a_ln_g, conv_a_ln_b, conv_b_w, w_out, norm_ffn_pre, norm_ffn_post, w_gate_up, w_down):
    params = dict(
        norm_mix_pre=norm_mix_pre, norm_mix_post=norm_mix_post, w_in=w_in, conv_a_w=conv_a_w,
        conv_a_b=conv_a_b, conv_a_ln_g=conv_a_ln_g, conv_a_ln_b=conv_a_ln_b, conv_b_w=conv_b_w,
        w_out=w_out, norm_ffn_pre=norm_ffn_pre, norm_ffn_post=norm_ffn_post,
        w_gate_up=w_gate_up, w_down=w_down)
    depth = cache_conv_a.shape[0]
    bp = x_prompt.shape[0]
    zeros_a = jnp.zeros((depth, bp) + cache_conv_a.shape[2:], x_prompt.dtype)
    zeros_b = jnp.zeros((depth, bp) + cache_conv_b.shape[2:], x_prompt.dtype)
    y_p, a_p, b_p = _run(x_prompt, zeros_a, zeros_b, params, seqs_per_block=1, rows_per_seq=512)
    y_s, a_s, b_s = _run(x_sample, cache_conv_a, cache_conv_b, params,
                         seqs_per_block=8, rows_per_seq=x_sample.shape[1])
    return (y_p, y_s, a_p, b_p, a_s, b_s)
```

```python
import functools

import jax
import jax.numpy as jnp
from jax import lax
from jax.experimental import pallas as pl
from jax.experimental.pallas import tpu as pltpu

EPS = 1e-6
CONV_A_WIDTH = 31
CONV_B_WIDTH = 3
HIST_A = CONV_A_WIDTH - 1
HIST_B = CONV_B_WIDTH - 1
SUBLANES = 8
LANES = 128
HEAD_A = 32
HEAD_B = 8
CONV_ROWS = 32
MXU_COLS = 256
VMEM_LIMIT_BYTES = 60 * 1024 * 1024


def _rms(x, g):
    ms = jnp.mean(x * x, axis=-1, keepdims=True)
    return x * lax.rsqrt(ms + EPS) * g


def _taps(buf, blk, w8_ref, base, rows, lanes, width):
    acc = None
    for k in range(width):
        slab = buf[blk, pl.ds(base + k, rows), :].reshape(rows // SUBLANES, SUBLANES, LANES)
        term = slab * w8_ref[k][:, lanes][None]
        acc = term if acc is None else acc + term
    return acc.reshape(rows, LANES)


def _layer_kernel(xc_ref, xp_ref, ha_ref, hb_ref, g1_ref, g2_ref, win_ref, caw_ref, cab_ref,
                  lng_ref, lnb_ref, cbw_ref, wout_ref, g3_ref, g4_ref, wgu_ref, wd_ref,
                  y_ref, na_ref, nb_ref,
                  pbuf, abuf, cbuf, hbuf, hbuf2, mixbuf, actbuf, *, n_tiles, tiles_per_seq):
    n = pl.program_id(0)
    slot = n % 2
    nb, T, D = xc_ref.shape
    R = nb * T
    d_a = ha_ref.shape[2]
    d_b = hb_ref.shape[2]
    la = d_a // LANES
    lb = d_b // LANES
    d_ff = wd_ref.shape[0]
    f32 = jnp.float32
    bf16 = jnp.bfloat16

    def project_in():
        hbuf[...] = _rms(xc_ref[...].reshape(R, D), g1_ref[...]).astype(bf16)
        for n0 in range(0, win_ref.shape[1], 2 * MXU_COLS):
            pbuf[:, n0:n0 + 2 * MXU_COLS] = jnp.dot(
                hbuf[...], win_ref[:, n0:n0 + 2 * MXU_COLS], preferred_element_type=f32)

    cps = T // CONV_ROWS
    n_chunks = nb * cps

    def headers(seq_start):
        for s in range(nb):
            for c in range(la):
                head = ha_ref[s][:, c * LANES:(c + 1) * LANES]
                if seq_start is not None:
                    head = jnp.where(seq_start, head, abuf[s * la + c, T:T + HEAD_A, :])
                abuf[s * la + c, 0:HEAD_A, :] = head
            for c in range(lb):
                head = hb_ref[s][:, c * LANES:(c + 1) * LANES]
                if seq_start is not None:
                    head = jnp.where(seq_start, head, cbuf[s * lb + c, T:T + HEAD_B, :])
                cbuf[s * lb + c, 0:HEAD_B, :] = head

    def tails():
        for s in range(nb):
            for c in range(la):
                na_ref[s, :, c * LANES:(c + 1) * LANES] = abuf[s * la + c, HEAD_A + T - HIST_A:HEAD_A + T, :]
            for c in range(lb):
                nb_ref[s, :, c * LANES:(c + 1) * LANES] = cbuf[s * lb + c, HEAD_B + T - HIST_B:HEAD_B + T, :]

    def front_chunk(j):
        s, ch = divmod(j, cps)
        t0 = ch * CONV_ROWS
        rows = slice(s * T + t0, s * T + t0 + CONV_ROWS)
        for c in range(la):
            lanes = slice(c * LANES, (c + 1) * LANES)
            abuf[s * la + c, HEAD_A + t0:HEAD_A + t0 + CONV_ROWS, :] = (
                pbuf[rows, lanes] * jax.nn.sigmoid(pbuf[rows, d_a + c * LANES:d_a + (c + 1) * LANES]))
        for c in range(lb):
            c0 = 2 * d_a + d_b + c * LANES
            cbuf[s * lb + c, HEAD_B + t0:HEAD_B + t0 + CONV_ROWS, :] = (
                pbuf[rows, c0:c0 + LANES] * pbuf[rows, c0 + d_b:c0 + d_b + LANES])

        ya = [_taps(abuf, s * la + c, caw_ref, t0 + HEAD_A - HIST_A, CONV_ROWS,
                    slice(c * LANES, (c + 1) * LANES), CONV_A_WIDTH) for c in range(la)]
        ya = jnp.concatenate(ya, axis=1) + cab_ref[...]
        mu = jnp.mean(ya, axis=-1, keepdims=True)
        yc = ya - mu
        ln = yc * lax.rsqrt(jnp.mean(yc * yc, axis=-1, keepdims=True) + EPS)
        ln = ln * lng_ref[...] + lnb_ref[...]
        mixbuf[slot, rows, 0:d_a] = (ln * jax.nn.sigmoid(ln)).astype(bf16)
        yb = [_taps(cbuf, s * lb + c, cbw_ref, t0 + HEAD_B - HIST_B, CONV_ROWS,
                    slice(c * LANES, (c + 1) * LANES), CONV_B_WIDTH) for c in range(lb)]
        yb = jnp.concatenate(yb, axis=1)
        mixbuf[slot, rows, d_a:d_a + d_b] = (pbuf[rows, 2 * d_a:2 * d_a + d_b] * yb).astype(bf16)

    def front_half(seq_start):
        headers(seq_start)
        for j in range(n_chunks):
            front_chunk(j)
        tails()

    def project_out():
        m = jnp.dot(mixbuf[1 - slot], wout_ref[...], preferred_element_type=f32)
        y_ref[...] = (xp_ref[...].reshape(R, D) + _rms(m, g2_ref[...])).reshape(nb, T, D)
        hbuf2[...] = _rms(y_ref[...].reshape(R, D), g3_ref[...]).astype(bf16)

    def ffn():
        for c in range(d_ff // MXU_COLS):
            gu = jnp.dot(hbuf2[...], wgu_ref[:, 2 * c * MXU_COLS:2 * (c + 1) * MXU_COLS],
                         preferred_element_type=f32)
            g = gu[:, :MXU_COLS]
            u = gu[:, MXU_COLS:]
            actbuf[:, c * MXU_COLS:(c + 1) * MXU_COLS] = (g * jax.nn.sigmoid(g) * u).astype(bf16)
        for h in range(2):
            f = jnp.dot(actbuf[h * (R // 2):(h + 1) * (R // 2), :], wd_ref[...], preferred_element_type=f32)
            out = _rms(f, g4_ref[...])
            if nb == 1:
                half = (0, slice(h * (T // 2), (h + 1) * (T // 2)))
                y_ref[half] = y_ref[half] + out
            else:
                half = slice(h * (nb // 2), (h + 1) * (nb // 2))
                y_ref[half] = y_ref[half] + out.reshape(nb // 2, T, D)

    @pl.when(n == 0)
    def _():
        project_in()
        front_half(None)

    @pl.when(jnp.logical_and(n > 0, n < n_tiles))
    def _():
        project_in()
        project_out()
        ffn()
        front_half(None if tiles_per_seq == 1 else n % tiles_per_seq == 0)

    @pl.when(n == n_tiles)
    def _():
        project_out()
        ffn()


def _pair_gate_up(w_gate_up):
    d, two_ff = w_gate_up.shape
    blocks = two_ff // (2 * MXU_COLS)
    return w_gate_up.reshape(d, 2, blocks, MXU_COLS).transpose(0, 2, 1, 3).reshape(d, two_ff)


def _resident(shape):
    return pl.BlockSpec(shape, lambda n: (0,) * len(shape), pipeline_mode=pl.Buffered(1))


def _layer(x, hist_a, hist_b, p, *, seqs_per_block, rows_per_seq):
    B, S, D = x.shape
    nb, T = seqs_per_block, rows_per_seq
    assert B % nb == 0 and S % T == 0 and T % CONV_ROWS == 0 and T >= HEAD_A
    d_a = hist_a.shape[-1]
    d_b = hist_b.shape[-1]
    d_in = p["w_in"].shape[1]
    d_ff = p["w_down"].shape[0]
    assert d_in % (2 * MXU_COLS) == 0 and d_a % LANES == 0 and d_b % LANES == 0
    assert d_ff % MXU_COLS == 0
    R = nb * T
    tps = S // T
    n_tiles = (B // nb) * tps
    f32 = jnp.float32
    bf16 = jnp.bfloat16

    ha = jnp.pad(hist_a, ((0, 0), (HEAD_A - HIST_A, 0), (0, 0)))
    hb = jnp.pad(hist_b, ((0, 0), (HEAD_B - HIST_B, 0), (0, 0)))
    row = lambda v: v.reshape(1, -1)
    bcast8 = lambda w: jnp.broadcast_to(w[:, None, :], (w.shape[0], SUBLANES, w.shape[1]))

    args = (
        x, x, ha, hb, row(p["norm_mix_pre"]), row(p["norm_mix_post"]), p["w_in"].astype(bf16),
        bcast8(p["conv_a_w"]), row(p["conv_a_b"]), row(p["conv_a_ln_g"]), row(p["conv_a_ln_b"]),
        bcast8(p["conv_b_w"]), p["w_out"].astype(bf16), row(p["norm_ffn_pre"]),
        row(p["norm_ffn_post"]), _pair_gate_up(p["w_gate_up"]).astype(bf16), p["w_down"].astype(bf16))

    cur = lambda n: jnp.minimum(n, n_tiles - 1)
    prev = lambda n: jnp.maximum(n - 1, 0)
    cur_tile = lambda n: (cur(n) // tps, cur(n) % tps, 0)
    prev_tile = lambda n: (prev(n) // tps, prev(n) % tps, 0)
    cur_seq = lambda n: (cur(n) // tps, 0, 0)
    in_specs = [
        pl.BlockSpec((nb, T, D), cur_tile),
        pl.BlockSpec((nb, T, D), prev_tile),
        pl.BlockSpec((nb, HEAD_A, d_a), cur_seq),
        pl.BlockSpec((nb, HEAD_B, d_b), cur_seq),
    ] + [_resident(a.shape) for a in args[4:]]
    out_specs = [
        pl.BlockSpec((nb, T, D), prev_tile),
        pl.BlockSpec((nb, HIST_A, d_a), cur_seq),
        pl.BlockSpec((nb, HIST_B, d_b), cur_seq),
    ]
    out_shape = [
        jax.ShapeDtypeStruct((B, S, D), x.dtype),
        jax.ShapeDtypeStruct((B, HIST_A, d_a), x.dtype),
        jax.ShapeDtypeStruct((B, HIST_B, d_b), x.dtype),
    ]
    scratch = [
        pltpu.VMEM((R, d_in), f32),
        pltpu.VMEM((nb * (d_a // LANES), HEAD_A + T, LANES), f32),
        pltpu.VMEM((nb * (d_b // LANES), HEAD_B + T, LANES), f32),
        pltpu.VMEM((R, D), bf16),
        pltpu.VMEM((R, D), bf16),
        pltpu.VMEM((2, R, d_a + d_b), bf16),
        pltpu.VMEM((R, d_ff), bf16),
    ]
    return pl.pallas_call(
        functools.partial(_layer_kernel, n_tiles=n_tiles, tiles_per_seq=tps),
        grid=(n_tiles + 1,),
        in_specs=in_specs, out_specs=out_specs, out_shape=out_shape,
        scratch_shapes=scratch,
        compiler_params=pltpu.CompilerParams(
            dimension_semantics=("arbitrary",),
            vmem_limit_bytes=VMEM_LIMIT_BYTES),
        name="encoder_layer",
    )(*args)


def _run(x, hists_a, hists_b, params, *, seqs_per_block, rows_per_seq):
    depth = hists_a.shape[0]
    new_as, new_bs = [], []
    for l in range(depth):
        p = {k: v[l] for k, v in params.items()}
        x, na, nb = _layer(x, hists_a[l], hists_b[l], p,
                           seqs_per_block=seqs_per_block, rows_per_seq=rows_per_seq)
        new_as.append(na)
        new_bs.append(nb)
    return x, jnp.stack(new_as), jnp.stack(new_bs)


def kernel(x_prompt, x_sample, cache_conv_a, cache_conv_b, norm_mix_pre, norm_mix_post, w_in, conv_a_w, conv_a_b, conv_a_ln_g, conv_a_ln_b, conv_b_w, w_out, norm_ffn_pre, norm_ffn_post, w_gate_up, w_down):
    params = dict(
        norm_mix_pre=norm_mix_pre, norm_mix_post=norm_mix_post, w_in=w_in, conv_a_w=conv_a_w,
        conv_a_b=conv_a_b, conv_a_ln_g=conv_a_ln_g, conv_a_ln_b=conv_a_ln_b, conv_b_w=conv_b_w,
        w_out=w_out, norm_ffn_pre=norm_ffn_pre, norm_ffn_post=norm_ffn_post,
        w_gate_up=w_gate_up, w_down=w_down)
    depth = cache_conv_a.shape[0]
    bp = x_prompt.shape[0]
    zeros_a = jnp.zeros((depth, bp) + cache_conv_a.shape[2:], x_prompt.dtype)
    zeros_b = jnp.zeros((depth, bp) + cache_conv_b.shape[2:], x_prompt.dtype)
    y_p, a_p, b_p = _run(x_prompt, zeros_a, zeros_b, params, seqs_per_block=1, rows_per_seq=512)
    y_s, a_s, b_s = _run(x_sample, cache_conv_a, cache_conv_b, params,
                         seqs_per_block=8, rows_per_seq=x_sample.shape[1])
    return (y_p, y_s, a_p, b_p, a_s, b_s)
```
